```python
import math
import jax, jax.numpy as jnp
from jax import lax
import numpy as np

D_MODEL = 1024
BATCH = 8
SEQ = 4096
DEPTH = 2
DEC_BATCH = 32
DEC_SEQ = 1
PAST_LEN = 16384
PAGE_SIZE = 128

N_HEADS = 8
HEAD_DIM = D_MODEL // N_HEADS
ROT_DIM = HEAD_DIM // 4
ROPE_THETA = 500000.0
BLOCK = 256
TOP_K = 3
Q_CHUNK = 32
CONV_W = 31
D_FF = 4 * D_MODEL
N_A = DEPTH // 2
N_B = DEPTH - N_A
EPS = 1e-6
ATTN_SCALE = HEAD_DIM ** -0.5

kernel_name = "yoco_conformer_moba_step"


def rmsnorm(x, g):
    xf = x.astype(jnp.float32)
    y = xf * lax.rsqrt(jnp.mean(xf * xf, axis=-1, keepdims=True) + EPS)
    return (y * g.astype(jnp.float32)).astype(x.dtype)


def layernorm(x, g, b):
    xf = x.astype(jnp.float32)
    mu = jnp.mean(xf, axis=-1, keepdims=True)
    var = jnp.mean(jnp.square(xf - mu), axis=-1, keepdims=True)
    y = (xf - mu) * lax.rsqrt(var + EPS)
    return (y * g.astype(jnp.float32) + b.astype(jnp.float32)).astype(x.dtype)


def rope(x, pos):
    half = ROT_DIM // 2
    inv = 1.0 / (ROPE_THETA ** (jnp.arange(half, dtype=jnp.float32) * 2.0 / ROT_DIM))
    ang = pos.astype(jnp.float32)[:, None] * inv[None, :]
    cos = jnp.cos(ang)[None, :, None, :]
    sin = jnp.sin(ang)[None, :, None, :]
    xr = x[..., :ROT_DIM].astype(jnp.float32)
    x1, x2 = xr[..., :half], xr[..., half:]
    rot = jnp.concatenate([x1 * cos - x2 * sin, x2 * cos + x1 * sin], axis=-1).astype(x.dtype)
    return jnp.concatenate([rot, x[..., ROT_DIM:]], axis=-1)


def sq_relu_mlp(x, w_up, w_down):
    return jnp.square(jax.nn.relu(x @ w_up)) @ w_down


def conv_module(x, hist, w_in, b_in, w_dw, b_dw, g_ln, b_ln, w_out, b_out):
    u = x @ w_in + b_in
    a, gate = jnp.split(u, 2, axis=-1)
    z = a * jax.nn.sigmoid(gate)
    cat = jnp.concatenate([hist.astype(z.dtype), z], axis=1)
    y = lax.conv_general_dilated(cat, w_dw[:, None, :].astype(z.dtype), (1,), 'VALID',
                                 dimension_numbers=('NWC', 'WIO', 'NWC'),
                                 feature_group_count=D_MODEL) + b_dw
    y = jax.nn.silu(layernorm(y, g_ln, b_ln))
    return y @ w_out + b_out, cat[:, -(CONV_W - 1):]


def moba_chunk(q, q_pos, k_blk, v_blk, k_mean):
    nq = q.shape[0]
    nb = k_blk.shape[0]
    own = q_pos // BLOCK
    gate = jnp.einsum('qhd,nhd->qhn', q.astype(jnp.float32), k_mean)
    past = jnp.arange(nb)[None, None, :] < own[:, None, None]
    gate = jnp.where(past, gate, -jnp.inf)
    k_sel = min(TOP_K, nb)
    _, top = lax.top_k(gate, k_sel)
    valid = jnp.broadcast_to(jnp.arange(k_sel)[None, None, :] < own[:, None, None],
                             (nq, N_HEADS, k_sel))
    own_b = jnp.broadcast_to(own[:, None, None], (nq, N_HEADS, 1))
    sel = jnp.concatenate([top.astype(jnp.int32), own_b.astype(jnp.int32)], axis=-1)
    ok = jnp.concatenate([valid, jnp.ones((nq, N_HEADS, 1), bool)], axis=-1)
    hh = jnp.arange(N_HEADS)[None, :, None]
    kg = k_blk[sel, :, hh]
    vg = v_blk[sel, :, hh]
    s = jnp.einsum('qhd,qhjkd->qhjk', q, kg, preferred_element_type=jnp.float32) * ATTN_SCALE
    k_pos = sel[..., None] * BLOCK + jnp.arange(BLOCK, dtype=jnp.int32)
    mask = ok[..., None] & (k_pos <= q_pos[:, None, None, None])
    s = jnp.where(mask, s, -jnp.inf)
    p = jax.nn.softmax(s.reshape(nq, N_HEADS, -1), axis=-1).reshape(s.shape).astype(vg.dtype)
    return jnp.einsum('qhjk,qhjkd->qhd', p, vg)


def moba(q, start, kb, vb, k_mean):
    b_sz, sq = q.shape[0], q.shape[1]
    qc = Q_CHUNK if sq % Q_CHUNK == 0 else sq
    nqc = sq // qc
    q_items = q.reshape(b_sz * nqc, qc, N_HEADS, HEAD_DIM)

    def step(args):
        qi, i = args
        b = i // nqc
        c = i % nqc
        pos = start + c * qc + jnp.arange(qc, dtype=jnp.int32)
        kb_b = lax.dynamic_index_in_dim(kb, b, 0, keepdims=False)
        vb_b = lax.dynamic_index_in_dim(vb, b, 0, keepdims=False)
        km_b = lax.dynamic_index_in_dim(k_mean, b, 0, keepdims=False)
        return moba_chunk(qi, pos, kb_b, vb_b, km_b)

    out = lax.map(step, (q_items, jnp.arange(b_sz * nqc, dtype=jnp.int32)))
    return out.reshape(b_sz, sq, N_HEADS * HEAD_DIM)


def trunk(x, start, conv_hist, k_past, v_past, g_mix, a_w_in, a_b_in, a_w_dw, a_b_dw,
          a_g_ln, a_b_ln, a_w_out, a_b_out, g_kv, w_k, w_v, g_k, w_q, g_q, w_o,
          g_mlp, w_up, w_down):
    b_sz, t = x.shape[0], x.shape[1]
    pos = start + jnp.arange(t, dtype=jnp.int32)
    h = x
    conv_new = []
    for l in range(N_A):
        o, st = conv_module(rmsnorm(h, g_mix[l]), conv_hist[l], a_w_in[l], a_b_in[l], a_w_dw[l],
                            a_b_dw[l], a_g_ln[l], a_b_ln[l], a_w_out[l], a_b_out[l])
        h = h + o
        conv_new.append(st)
        h = h + sq_relu_mlp(rmsnorm(h, g_mlp[l]), w_up[l], w_down[l])
    kv_x = rmsnorm(h, g_kv)
    k = rope(rmsnorm((kv_x @ w_k).reshape(b_sz, t, N_HEADS, HEAD_DIM), g_k), pos)
    v = (kv_x @ w_v).reshape(b_sz, t, N_HEADS, HEAD_DIM)
    total = start + t
    pad = (-total) % BLOCK
    nb = (total + pad) // BLOCK
    zpad = jnp.zeros((b_sz, pad, N_HEADS, HEAD_DIM), k.dtype)
    k_parts = [] if k_past is None else [k_past.astype(k.dtype)]
    v_parts = [] if v_past is None else [v_past.astype(v.dtype)]
    kb = jnp.concatenate(k_parts + [k, zpad], axis=1).reshape(b_sz, nb, BLOCK, N_HEADS, HEAD_DIM)
    vb = jnp.concatenate(v_parts + [v, zpad.astype(v.dtype)], axis=1).reshape(b_sz, nb, BLOCK, N_HEADS, HEAD_DIM)
    k_mean = jnp.mean(kb.astype(jnp.float32), axis=2)
    for j in range(N_B):
        l = N_A + j
        q = rope(rmsnorm((rmsnorm(h, g_mix[l]) @ w_q[j]).reshape(b_sz, t, N_HEADS, HEAD_DIM), g_q[j]), pos)
        h = h + moba(q, start, kb, vb, k_mean) @ w_o[j]
        h = h + sq_relu_mlp(rmsnorm(h, g_mlp[l]), w_up[l], w_down[l])
    return h, k, v, jnp.stack(conv_new)


def setup_inputs(seed: int = 0) -> dict:
    key = jax.random.key(seed)
    ks = jax.random.split(key, 26)
    f32 = jnp.float32
    n_pages = PAST_LEN // PAGE_SIZE
    n_used = DEC_BATCH * n_pages
    n_pool = n_used + max(1, n_used // 4)
    nrm = lambda k, s, sc: jax.random.normal(k, s, f32) * sc
    page_table = jax.random.permutation(ks[0], n_pool)[:n_used].reshape(DEC_BATCH, n_pages).astype(jnp.int32)
    return {
        "x_prompt": nrm(ks[1], (BATCH, SEQ, D_MODEL), 1.0),
        "x_sample": nrm(ks[2], (DEC_BATCH, DEC_SEQ, D_MODEL), 1.0),
        "cache_k": nrm(ks[3], (n_pool, PAGE_SIZE, N_HEADS, HEAD_DIM), 1.0),
        "cache_v": nrm(ks[4], (n_pool, PAGE_SIZE, N_HEADS, HEAD_DIM), 1.0),
        "state_conv": nrm(ks[5], (N_A, DEC_BATCH, CONV_W - 1, D_MODEL), 0.5),
        "page_table": page_table,
        "g_mix": 1.0 + nrm(ks[6], (DEPTH, D_MODEL), 0.02),
        "a_w_in": nrm(ks[7], (N_A, D_MODEL, 2 * D_MODEL), D_MODEL ** -0.5),
        "a_b_in": nrm(ks[8], (N_A, 2 * D_MODEL), 0.02),
        "a_w_dw": nrm(ks[9], (N_A, CONV_W, D_MODEL), CONV_W ** -0.5),
        "a_b_dw": nrm(ks[10], (N_A, D_MODEL), 0.02),
        "a_g_ln": 1.0 + nrm(ks[11], (N_A, D_MODEL), 0.02),
        "a_b_ln": nrm(ks[12], (N_A, D_MODEL), 0.02),
        "a_w_out": nrm(ks[13], (N_A, D_MODEL, D_MODEL), D_MODEL ** -0.5),
        "a_b_out": nrm(ks[14], (N_A, D_MODEL), 0.02),
        "g_kv": 1.0 + nrm(ks[15], (D_MODEL,), 0.02),
        "w_k": nrm(ks[16], (D_MODEL, N_HEADS * HEAD_DIM), D_MODEL ** -0.5),
        "w_v": nrm(ks[17], (D_MODEL, N_HEADS * HEAD_DIM), D_MODEL ** -0.5),
        "g_k": 1.0 + nrm(ks[18], (HEAD_DIM,), 0.02),
        "w_q": nrm(ks[19], (N_B, D_MODEL, N_HEADS * HEAD_DIM), D_MODEL ** -0.5),
        "g_q": 1.0 + nrm(ks[20], (N_B, HEAD_DIM), 0.02),
        "w_o": nrm(ks[21], (N_B, N_HEADS * HEAD_DIM, D_MODEL), D_MODEL ** -0.5),
        "g_mlp": 1.0 + nrm(ks[22], (DEPTH, D_MODEL), 0.02),
        "w_up": nrm(ks[23], (DEPTH, D_MODEL, D_FF), D_MODEL ** -0.5),
        "w_down": nrm(ks[24], (DEPTH, D_FF, D_MODEL), D_FF ** -0.5),
    }


def reference(x_prompt, x_sample, cache_k, cache_v, state_conv, page_table, g_mix, a_w_in,
              a_b_in, a_w_dw, a_b_dw, a_g_ln, a_b_ln, a_w_out, a_b_out, g_kv, w_k, w_v, g_k,
              w_q, g_q, w_o, g_mlp, w_up, w_down):
    dec_b, n_pages = page_table.shape
    page = cache_k.shape[1]
    past_len = n_pages * page
    k_past = cache_k[page_table].reshape(dec_b, past_len, N_HEADS, HEAD_DIM)
    v_past = cache_v[page_table].reshape(dec_b, past_len, N_HEADS, HEAD_DIM)
    conv0 = jnp.zeros((N_A, x_prompt.shape[0], CONV_W - 1, D_MODEL), x_prompt.dtype)
    y_prompt, k_prompt, v_prompt, conv_prompt = trunk(
        x_prompt, 0, conv0, None, None, g_mix, a_w_in, a_b_in, a_w_dw, a_b_dw, a_g_ln, a_b_ln,
        a_w_out, a_b_out, g_kv, w_k, w_v, g_k, w_q, g_q, w_o, g_mlp, w_up, w_down)
    y_sample, k_sample, v_sample, conv_sample = trunk(
        x_sample, past_len, state_conv, k_past, v_past, g_mix, a_w_in, a_b_in, a_w_dw, a_b_dw,
        a_g_ln, a_b_ln, a_w_out, a_b_out, g_kv, w_k, w_v, g_k, w_q, g_q, w_o, g_mlp, w_up, w_down)
    return (y_prompt, y_sample, k_prompt, v_prompt, conv_prompt, k_sample, v_sample, conv_sample)
```

```python
import functools

import jax
import jax.numpy as jnp
from jax import lax
from jax.experimental import pallas as pl
from jax.experimental.pallas import tpu as pltpu

F32 = jnp.float32
BF16 = jnp.bfloat16

D_MODEL = 1024
N_HEADS = 8
HEAD_DIM = D_MODEL // N_HEADS
ROT_DIM = HEAD_DIM // 4
ROPE_THETA = 500000.0
BLOCK = 256
TOP_K = 3
CONV_W = 31
D_FF = 4 * D_MODEL
EPS = 1e-6
ATTN_SCALE = HEAD_DIM ** -0.5
MASKED = -1e30

HALO = 32
CONV_ROWS = 32
CONV_LANES = 256
KMEAN_PAGES = 16
VMEM_LIMIT = 56 * 1024 * 1024


def _rms(x, g):
    return x * lax.rsqrt(jnp.mean(x * x, axis=-1, keepdims=True) + EPS) * g


def _const_spec(shape):
    nd = len(shape)
    return pl.BlockSpec(shape, lambda *_: (0,) * nd, pipeline_mode=pl.Buffered(1))


def _row_tile(m, pref):
    t = min(m, pref)
    while m % t:
        t //= 2
    return t


def _params(*sem):
    return pltpu.CompilerParams(dimension_semantics=sem, vmem_limit_bytes=VMEM_LIMIT)


def _conv_in_kernel(x_ref, g_ref, w_ref, b_ref, z_ref):
    xn = _rms(x_ref[...], g_ref[...]).astype(BF16)
    u = jnp.dot(xn, w_ref[...], preferred_element_type=F32) + b_ref[...]
    z_ref[...] = u[:, :D_MODEL] * jax.nn.sigmoid(u[:, D_MODEL:])


def _conv_in(x2d, g, w_in, b_in):
    m = x2d.shape[0]
    tm = _row_tile(m, 512)
    return pl.pallas_call(
        _conv_in_kernel,
        grid=(m // tm,),
        in_specs=[
            pl.BlockSpec((tm, D_MODEL), lambda i: (i, 0)),
            _const_spec((1, D_MODEL)),
            _const_spec((D_MODEL, 2 * D_MODEL)),
            _const_spec((1, 2 * D_MODEL)),
        ],
        out_specs=pl.BlockSpec((tm, D_MODEL), lambda i: (i, 0)),
        out_shape=jax.ShapeDtypeStruct((m, D_MODEL), F32),
        compiler_params=_params("parallel"),
        name="conv_in",
    )(x2d, g, w_in, b_in)


def _ln_swish(y, g, b):
    mu = jnp.mean(y, axis=-1, keepdims=True)
    d = y - mu
    var = jnp.mean(d * d, axis=-1, keepdims=True)
    yn = d * lax.rsqrt(var + EPS) * g + b
    return yn * jax.nn.sigmoid(yn)


def _dwconv_kernel(z_ref, zprev_ref, w_ref, bdw_ref, gln_ref, bln_ref, c_ref, buf, ybuf, *, tt):
    first = pl.program_id(1) == 0
    buf[0:HALO, :] = jnp.where(first, 0.0, zprev_ref[0])
    buf[HALO:, :] = z_ref[0]
    first_off = HALO - (CONV_W - 1)

    def row_block(rb, carry):
        t0 = pl.multiple_of(rb * CONV_ROWS, CONV_ROWS)
        for lb in range(D_MODEL // CONV_LANES):
            lanes = slice(lb * CONV_LANES, (lb + 1) * CONV_LANES)
            win = buf[pl.ds(t0, CONV_ROWS + HALO), lanes]
            acc = jnp.zeros((CONV_ROWS, CONV_LANES), F32)
            for s in range(8):
                shifted = win if s == 0 else win[s:s + CONV_ROWS + HALO - 8]
                for a in range(HALO // 8 + 1):
                    o = 8 * a + s
                    if first_off <= o <= HALO:
                        j = o - first_off
                        acc = acc + w_ref[j:j + 1, lanes] * shifted[8 * a:8 * a + CONV_ROWS]
            ybuf[pl.ds(t0, CONV_ROWS), lanes] = acc
        return carry

    lax.fori_loop(0, tt // CONV_ROWS, row_block, 0)
    y = ybuf[...] + bdw_ref[...]
    c_ref[0] = _ln_swish(y, gln_ref[...], bln_ref[...]).astype(BF16)


def _dwconv_prompt(z3d, w_dw, b_dw, g_ln, b_ln):
    b, t, _ = z3d.shape
    tt = _row_tile(t, 512)
    assert tt % HALO == 0 and tt % CONV_ROWS == 0
    per = tt // HALO
    return pl.pallas_call(
        functools.partial(_dwconv_kernel, tt=tt),
        grid=(b, t // tt),
        in_specs=[
            pl.BlockSpec((1, tt, D_MODEL), lambda bi, i: (bi, i, 0)),
            pl.BlockSpec((1, HALO, D_MODEL), lambda bi, i: (bi, jnp.maximum(i * per - 1, 0), 0)),
            _const_spec((CONV_W, D_MODEL)),
            _const_spec((1, D_MODEL)),
            _const_spec((1, D_MODEL)),
            _const_spec((1, D_MODEL)),
        ],
        out_specs=pl.BlockSpec((1, tt, D_MODEL), lambda bi, i: (bi, i, 0)),
        out_shape=jax.ShapeDtypeStruct((b, t, D_MODEL), BF16),
        scratch_shapes=[pltpu.VMEM((tt + HALO, D_MODEL), F32), pltpu.VMEM((tt, D_MODEL), F32)],
        compiler_params=_params("parallel", "arbitrary"),
        name="dwconv_prompt",
    )(z3d, z3d, w_dw, b_dw, g_ln, b_ln)


def _dwconv_step_kernel(hist_ref, z_ref, w_ref, bdw_ref, gln_ref, bln_ref, c_ref):
    acc = w_ref[CONV_W - 1:CONV_W, :] * z_ref[...]
    for j in range(CONV_W - 1):
        acc = acc + w_ref[j:j + 1, :] * hist_ref[j]
    c_ref[...] = _ln_swish(acc + bdw_ref[...], gln_ref[...], bln_ref[...]).astype(BF16)


def _dwconv_step(hist_t, z2d, w_dw, b_dw, g_ln, b_ln):
    m = z2d.shape[0]
    return pl.pallas_call(
        _dwconv_step_kernel,
        out_shape=jax.ShapeDtypeStruct((m, D_MODEL), BF16),
        compiler_params=pltpu.CompilerParams(vmem_limit_bytes=VMEM_LIMIT),
        name="dwconv_step",
    )(hist_t, z2d, w_dw, b_dw, g_ln, b_ln)


def _head_norm_rope(p, g, cos, sin_lo, sin_hi, out_ref):
    half = ROT_DIM // 2
    for h in range(N_HEADS):
        lanes = slice(h * HEAD_DIM, (h + 1) * HEAD_DIM)
        y = _rms(p[:, lanes], g)
        out_ref[:, lanes] = (y * cos + pltpu.roll(y, HEAD_DIM - half, 1) * sin_lo
                             + pltpu.roll(y, half, 1) * sin_hi)


def _dense_kernel(*refs, has_bias, has_kvq, fc):
    it = iter(refs)
    a_ref, wa_ref = next(it), next(it)
    ba_ref = next(it) if has_bias else None
    res_ref, gmlp_ref, wup_ref, wdn_ref = next(it), next(it), next(it), next(it)
    if has_kvq:
        (gkv_ref, wk_ref, wv_ref, gk_ref, gmix_ref, wq_ref, gq_ref,
         cos_ref, slo_ref, shi_ref) = [next(it) for _ in range(10)]
    h_ref = next(it)

    h = res_ref[...] + jnp.dot(a_ref[...], wa_ref[...], preferred_element_type=F32)
    if has_bias:
        h = h + ba_ref[...]
    hn = _rms(h, gmlp_ref[...]).astype(BF16)
    acc = h
    for c in range(D_FF // fc):
        u = jnp.dot(hn, wup_ref[:, c * fc:(c + 1) * fc], preferred_element_type=F32)
        u = jnp.square(jnp.maximum(u, 0.0)).astype(BF16)
        acc = acc + jnp.dot(u, wdn_ref[c * fc:(c + 1) * fc, :], preferred_element_type=F32)
    h_ref[...] = acc
    if has_kvq:
        k_ref, v_ref, q_ref = next(it), next(it), next(it)
        cos, slo, shi = cos_ref[...], slo_ref[...], shi_ref[...]
        kvx = _rms(acc, gkv_ref[...]).astype(BF16)
        kp = jnp.dot(kvx, wk_ref[...], preferred_element_type=F32)
        _head_norm_rope(kp, gk_ref[...], cos, slo, shi, k_ref)
        v_ref[...] = jnp.dot(kvx, wv_ref[...], preferred_element_type=F32)
        qx = _rms(acc, gmix_ref[...]).astype(BF16)
        qp = jnp.dot(qx, wq_ref[...], preferred_element_type=F32)
        _head_norm_rope(qp, gq_ref[...], cos, slo, shi, q_ref)


def _dense(a, w_a, b_a, res, g_mlp, w_up, w_dn, kvq=None):
    m = res.shape[0]
    tm = _row_tile(m, 256)
    row = pl.BlockSpec((tm, D_MODEL), lambda i: (i, 0))
    has_bias = b_a is not None
    args = [a, w_a] + ([b_a] if has_bias else []) + [res, g_mlp, w_up, w_dn]
    specs = ([row, _const_spec((D_MODEL, D_MODEL))] + ([_const_spec((1, D_MODEL))] if has_bias else [])
             + [row, _const_spec((1, D_MODEL)), _const_spec((D_MODEL, D_FF)), _const_spec((D_FF, D_MODEL))])
    out_shape = [jax.ShapeDtypeStruct((m, D_MODEL), F32)]
    out_specs = [row]
    if kvq is not None:
        g_kv, w_k, w_v, g_k, g_mix, w_q, g_q, cos, slo, shi = kvq
        nt = cos.shape[0] // tm
        tab = pl.BlockSpec((tm, HEAD_DIM), lambda i: (i % nt, 0))
        sq = _const_spec((D_MODEL, D_MODEL))
        args += [g_kv, w_k, w_v, g_k, g_mix, w_q, g_q, cos, slo, shi]
        specs += [_const_spec((1, D_MODEL)), sq, sq, _const_spec((1, HEAD_DIM)),
                  _const_spec((1, D_MODEL)), sq, _const_spec((1, HEAD_DIM)), tab, tab, tab]
        out_shape += [jax.ShapeDtypeStruct((m, D_MODEL), F32)] * 3
        out_specs += [row] * 3
    return pl.pallas_call(
        functools.partial(_dense_kernel, has_bias=has_bias, has_kvq=kvq is not None, fc=1024),
        grid=(m // tm,),
        in_specs=specs,
        out_specs=out_specs,
        out_shape=out_shape,
        compiler_params=_params("parallel"),
        name="dense_kvq" if kvq is not None else "dense",
    )(*args)


def _nt_dot(a, b, **kw):
    return lax.dot_general(a, b, (((1,), (1,)), ((), ())), preferred_element_type=F32, **kw)


def _moba_prompt_kernel(q_ref, k_ref, v_ref, o_ref, kmean_sc, m_sc, l_sc, acc_sc, *, nb):
    qi = pl.program_id(2)

    @pl.when(qi == 0)
    def _():
        for n in range(nb):
            kmean_sc[n:n + 1, :] = jnp.mean(k_ref[0, n * BLOCK:(n + 1) * BLOCK, :], axis=0, keepdims=True)
        if nb < HEAD_DIM:
            kmean_sc[nb:, :] = jnp.zeros((HEAD_DIM - nb, HEAD_DIM), F32)

    q = q_ref[0]
    gate = _nt_dot(kmean_sc[...], q, precision=lax.Precision.HIGHEST)
    row = lax.broadcasted_iota(jnp.int32, gate.shape, 0)
    gate = jnp.where(row < qi, gate, -jnp.inf)
    cnt = jnp.zeros(gate.shape, jnp.int32)
    for mth in range(nb - 1):
        gm = gate[mth:mth + 1, :]
        beats = (gm > gate) | ((gm == gate) & (mth < row))
        cnt = cnt + jnp.where(beats, 1, 0)
    sel = (row < qi) & (cnt < TOP_K)
    bias = jnp.where(sel, 0.0, MASKED).T

    qb = q.astype(BF16)
    own = pl.ds(pl.multiple_of(qi * BLOCK, BLOCK), BLOCK)
    s = _nt_dot(qb, k_ref[0, own, :].astype(BF16)) * ATTN_SCALE
    r_i = lax.broadcasted_iota(jnp.int32, s.shape, 0)
    c_i = lax.broadcasted_iota(jnp.int32, s.shape, 1)
    s = jnp.where(c_i <= r_i, s, MASKED)
    m0 = jnp.max(s, axis=-1, keepdims=True)
    p = jnp.exp(s - m0)
    m_sc[...] = m0
    l_sc[...] = jnp.sum(p, axis=-1, keepdims=True)
    acc_sc[...] = jnp.dot(p.astype(BF16), v_ref[0, own, :].astype(BF16), preferred_element_type=F32)

    for n in range(nb - 1):
        @pl.when(n < qi)
        def _(n=n):
            rows = slice(n * BLOCK, (n + 1) * BLOCK)
            sn = _nt_dot(qb, k_ref[0, rows, :].astype(BF16)) * ATTN_SCALE + bias[:, n:n + 1]
            m_old = m_sc[...]
            m_new = jnp.maximum(m_old, jnp.max(sn, axis=-1, keepdims=True))
            alpha = jnp.exp(m_old - m_new)
            pn = jnp.exp(sn - m_new)
            m_sc[...] = m_new
            l_sc[...] = alpha * l_sc[...] + jnp.sum(pn, axis=-1, keepdims=True)
            acc_sc[...] = alpha * acc_sc[...] + jnp.dot(
                pn.astype(BF16), v_ref[0, rows, :].astype(BF16), preferred_element_type=F32)

    o_ref[0] = (acc_sc[...] / l_sc[...]).astype(BF16)


def _moba_prompt(q3d, k3d, v3d):
    b, t, _ = q3d.shape
    assert t % BLOCK == 0
    nb = t // BLOCK
    assert nb <= HEAD_DIM
    qspec = pl.BlockSpec((1, BLOCK, HEAD_DIM), lambda bi, h, qi: (bi, qi, h))
    kvspec = pl.BlockSpec((1, t, HEAD_DIM), lambda bi, h, qi: (bi, 0, h))
    return pl.pallas_call(
        functools.partial(_moba_prompt_kernel, nb=nb),
        grid=(b, N_HEADS, nb),
        in_specs=[qspec, kvspec, kvspec],
        out_specs=qspec,
        out_shape=jax.ShapeDtypeStruct((b, t, D_MODEL), BF16),
        scratch_shapes=[pltpu.VMEM((HEAD_DIM, HEAD_DIM), F32), pltpu.VMEM((BLOCK, 1), F32),
                        pltpu.VMEM((BLOCK, 1), F32), pltpu.VMEM((BLOCK, HEAD_DIM), F32)],
        compiler_params=_params("parallel", "parallel", "arbitrary"),
        name="moba_prompt",
    )(q3d, k3d, v3d)


def _kmean_kernel(pt_ref, *refs, pages_per_block):
    del pt_ref
    page_refs, out_ref = refs[:-1], refs[-1]
    nblk = len(page_refs) // pages_per_block
    for j in range(nblk):
        tot = None
        for p in range(pages_per_block):
            part = jnp.sum(page_refs[j * pages_per_block + p][0], axis=0)
            tot = part if tot is None else tot + part
        out_ref[0, j] = tot * (1.0 / BLOCK)


def _kmean_pages(cache_k, pt_flat, dec_b, n_pages):
    page = cache_k.shape[1]
    ppb = BLOCK // page
    assert n_pages % KMEAN_PAGES == 0
    steps = n_pages // KMEAN_PAGES
    blk_per_step = KMEAN_PAGES // ppb
    nblk = n_pages // ppb

    def page_map(bi, i, pt, *, j):
        return (pt[bi * n_pages + i * KMEAN_PAGES + j], 0, 0, 0)

    return pl.pallas_call(
        functools.partial(_kmean_kernel, pages_per_block=ppb),
        grid_spec=pltpu.PrefetchScalarGridSpec(
            num_scalar_prefetch=1,
            grid=(dec_b, steps),
            in_specs=[pl.BlockSpec((1, page, N_HEADS, HEAD_DIM), functools.partial(page_map, j=j))
                      for j in range(KMEAN_PAGES)],
            out_specs=pl.BlockSpec((1, blk_per_step, N_HEADS, HEAD_DIM), lambda bi, i, pt: (bi, i, 0, 0)),
        ),
        out_shape=jax.ShapeDtypeStruct((dec_b, nblk, N_HEADS, HEAD_DIM), F32),
        compiler_params=_params("parallel", "arbitrary"),
        name="kmean_pages",
    )(pt_flat, *([cache_k] * KMEAN_PAGES))


def _select_kernel(q_ref, km_ref, idx_ref, *, nblk):
    q = q_ref[0]
    rown = lax.broadcasted_iota(jnp.int32, (nblk, nblk), 0)
    colm = lax.broadcasted_iota(jnp.int32, (nblk, nblk), 1)
    orow = lax.broadcasted_iota(jnp.int32, (N_HEADS, HEAD_DIM), 0)
    ocol = lax.broadcasted_iota(jnp.int32, (N_HEADS, HEAD_DIM), 1)
    ncol = lax.broadcasted_iota(jnp.int32, (nblk, 1), 0).astype(F32)
    out = jnp.zeros((N_HEADS, HEAD_DIM), F32)
    for h in range(N_HEADS):
        g = jnp.sum(km_ref[0, h] * q[h:h + 1, :], axis=1, keepdims=True)
        g_n = jnp.broadcast_to(g, (nblk, nblk))
        g_m = jnp.broadcast_to(jnp.sum(jnp.where(rown == colm, g_n, 0.0), axis=0, keepdims=True),
                               (nblk, nblk))
        beats = (g_m > g_n) | ((g_m == g_n) & (colm < rown))
        rank = jnp.sum(jnp.where(beats, 1.0, 0.0), axis=1, keepdims=True)
        for r in range(TOP_K):
            pick = jnp.sum(jnp.where(rank == r, ncol, 0.0), axis=0, keepdims=True)
            out = jnp.where((orow == h) & (ocol == r), pick, out)
    idx_ref[0] = out.astype(jnp.int32)


def _select_blocks(q_heads, kmean_hm):
    dec_b, _, nblk, _ = kmean_hm.shape
    assert nblk >= TOP_K
    return pl.pallas_call(
        functools.partial(_select_kernel, nblk=nblk),
        grid=(dec_b,),
        in_specs=[pl.BlockSpec((1, N_HEADS, HEAD_DIM), lambda bi: (bi, 0, 0)),
                  pl.BlockSpec((1, N_HEADS, nblk, HEAD_DIM), lambda bi: (bi, 0, 0, 0))],
        out_specs=pl.BlockSpec((1, N_HEADS, HEAD_DIM), lambda bi: (bi, 0, 0)),
        out_shape=jax.ShapeDtypeStruct((dec_b, N_HEADS, HEAD_DIM), jnp.int32),
        compiler_params=_params("parallel"),
        name="select_blocks",
    )(q_heads, kmean_hm)


def _gather_attn_kernel(idx_ref, pt_ref, q_ref, kn_ref, vn_ref, *refs, n_sel):
    del idx_ref, pt_ref
    kp, vp, o_ref = refs[:n_sel], refs[n_sel:2 * n_sel], refs[2 * n_sel]
    rows = 16
    h = pl.program_id(1)
    page = kp[0].shape[1] // N_HEADS
    head_rows = lambda ref: ref[0, pl.ds(h, page, stride=N_HEADS), :]
    q = q_ref[0]
    qb = jnp.broadcast_to(q, (rows, HEAD_DIM)).astype(BF16)
    ss = [_nt_dot(qb, head_rows(kp[j]).astype(BF16)) * ATTN_SCALE for j in range(n_sel)]
    q_r = q.astype(BF16).astype(F32)
    kn_r = kn_ref[0].astype(BF16).astype(F32)
    s_self = jnp.sum(q_r * kn_r, axis=-1, keepdims=True) * ATTN_SCALE
    m = s_self
    for sj in ss:
        m = jnp.maximum(m, jnp.max(sj, axis=-1, keepdims=True))
    p_self = jnp.exp(s_self - m)
    l = p_self
    acc = p_self * vn_ref[0].astype(BF16).astype(F32)
    for j, sj in enumerate(ss):
        pj = jnp.exp(sj - m)
        l = l + jnp.sum(pj, axis=-1, keepdims=True)
        acc = acc + jnp.dot(pj.astype(BF16), head_rows(vp[j]).astype(BF16), preferred_element_type=F32)
    o_ref[0] = (acc / l)[0:1]


def _gather_attn(q3d, kn3d, vn3d, cache_k, cache_v, idx_flat, pt_flat, n_pages):
    dec_b = q3d.shape[0]
    n_pool, page = cache_k.shape[0], cache_k.shape[1]
    ppb = BLOCK // page
    n_sel = TOP_K * ppb
    cache_k = cache_k.reshape(n_pool, page * N_HEADS, HEAD_DIM)
    cache_v = cache_v.reshape(n_pool, page * N_HEADS, HEAD_DIM)

    def page_map(bi, h, idx, pt, *, r, p):
        return (pt[bi * n_pages + idx[(bi * N_HEADS + h) * TOP_K + r] * ppb + p], 0, 0)

    tok = pl.BlockSpec((1, 1, HEAD_DIM), lambda bi, h, idx, pt: (bi, 0, h))
    pages = [pl.BlockSpec((1, page * N_HEADS, HEAD_DIM), functools.partial(page_map, r=r, p=p))
             for r in range(TOP_K) for p in range(ppb)]
    return pl.pallas_call(
        functools.partial(_gather_attn_kernel, n_sel=n_sel),
        grid_spec=pltpu.PrefetchScalarGridSpec(
            num_scalar_prefetch=2,
            grid=(dec_b, N_HEADS),
            in_specs=[tok, tok, tok] + pages + pages,
            out_specs=tok,
        ),
        out_shape=jax.ShapeDtypeStruct((dec_b, 1, D_MODEL), F32),
        compiler_params=_params("parallel", "arbitrary"),
        name="gather_attn",
    )(idx_flat, pt_flat, q3d, kn3d, vn3d, *([cache_k] * n_sel), *([cache_v] * n_sel))


def _rope_tables(pos):
    half = ROT_DIM // 2
    inv = 1.0 / (ROPE_THETA ** (jnp.arange(half, dtype=F32) * 2.0 / ROT_DIM))
    ang = pos.astype(F32)[:, None] * inv[None, :]
    cos, sin = jnp.cos(ang), jnp.sin(ang)
    n = pos.shape[0]
    cos_t = jnp.concatenate([cos, cos, jnp.ones((n, HEAD_DIM - ROT_DIM), F32)], axis=1)
    sin_lo = jnp.concatenate([-sin, jnp.zeros((n, HEAD_DIM - half), F32)], axis=1)
    sin_hi = jnp.concatenate([jnp.zeros((n, half), F32), sin, jnp.zeros((n, HEAD_DIM - ROT_DIM), F32)], axis=1)
    return cos_t, sin_lo, sin_hi


def kernel(x_prompt, x_sample, cache_k, cache_v, state_conv, page_table, g_mix, a_w_in,
           a_b_in, a_w_dw, a_b_dw, a_g_ln, a_b_ln, a_w_out, a_b_out, g_kv, w_k, w_v, g_k,
           w_q, g_q, w_o, g_mlp, w_up, w_down):
    assert x_prompt.shape[-1] == D_MODEL and g_mix.shape[0] == 2 and a_w_in.shape[0] == 1
    assert x_sample.shape[1] == 1
    bsz, seq, _ = x_prompt.shape
    dec_b, n_pages = page_table.shape
    page = cache_k.shape[1]
    assert cache_k.shape[2:] == (N_HEADS, HEAD_DIM) and seq >= CONV_W - 1
    past_len = n_pages * page
    assert BLOCK % page == 0 and past_len % BLOCK == 0

    row = lambda v: v.reshape(1, -1)
    bf = lambda w: w.astype(BF16)
    w_in, w_out, w_k_b, w_v_b, w_q_b, w_o_b = bf(a_w_in[0]), bf(a_w_out[0]), bf(w_k), bf(w_v), bf(w_q[0]), bf(w_o[0])
    w_up_b, w_dn_b = bf(w_up), bf(w_down)

    def layers(x2d, conv_fn, tables):
        z = _conv_in(x2d, row(g_mix[0]), w_in, row(a_b_in[0]))
        c = conv_fn(z)
        kvq = (row(g_kv), w_k_b, w_v_b, row(g_k), row(g_mix[1]), w_q_b, row(g_q[0])) + tables
        h, k, v, q = _dense(c, w_out, row(a_b_out[0]), x2d, row(g_mlp[0]), w_up_b[0], w_dn_b[0], kvq=kvq)
        return z, h, k, v, q

    conv_w = (a_w_dw[0], row(a_b_dw[0]), row(a_g_ln[0]), row(a_b_ln[0]))

    m_p = bsz * seq
    tables_p = _rope_tables(jnp.arange(seq, dtype=jnp.int32))
    conv_p = lambda z: _dwconv_prompt(z.reshape(bsz, seq, D_MODEL), *conv_w).reshape(m_p, D_MODEL)
    z_p, h_p, k_p, v_p, q_p = layers(x_prompt.reshape(m_p, D_MODEL), conv_p, tables_p)
    as3 = lambda t: t.reshape(bsz, seq, D_MODEL)
    o_p = _moba_prompt(as3(q_p), as3(k_p), as3(v_p)).reshape(m_p, D_MODEL)
    (y_p,) = _dense(o_p, w_o_b, None, h_p, row(g_mlp[1]), w_up_b[1], w_dn_b[1])

    pos_s = jnp.full((dec_b,), past_len, dtype=jnp.int32)
    tables_s = _rope_tables(pos_s)
    hist_t = jnp.transpose(state_conv[0], (1, 0, 2))
    conv_s = lambda z: _dwconv_step(hist_t, z, *conv_w)
    z_s, h_s, k_s, v_s, q_s = layers(x_sample.reshape(dec_b, D_MODEL), conv_s, tables_s)
    pt_flat = page_table.reshape(-1)
    s3 = lambda t: t.reshape(dec_b, 1, D_MODEL)
    kmean = _kmean_pages(cache_k, pt_flat, dec_b, n_pages)
    idx = _select_blocks(q_s.reshape(dec_b, N_HEADS, HEAD_DIM), jnp.transpose(kmean, (0, 2, 1, 3)))
    idx_flat = idx[:, :, :TOP_K].reshape(-1)
    o_s = _gather_attn(s3(q_s), s3(k_s), s3(v_s), cache_k, cache_v, idx_flat, pt_flat, n_pages)
    (y_s,) = _dense(o_s.reshape(dec_b, D_MODEL).astype(BF16), w_o_b, None, h_s, row(g_mlp[1]), w_up_b[1], w_dn_b[1])

    heads_p = lambda t: t.reshape(bsz, seq, N_HEADS, HEAD_DIM)
    heads_s = lambda t: t.reshape(dec_b, 1, N_HEADS, HEAD_DIM)
    conv_prompt = z_p.reshape(bsz, seq, D_MODEL)[:, seq - (CONV_W - 1):][None]
    conv_sample = jnp.concatenate([state_conv[0][:, 1:], z_s[:, None, :]], axis=1)[None]
    return (y_p.reshape(bsz, seq, D_MODEL), y_s.reshape(dec_b, 1, D_MODEL),
            heads_p(k_p), heads_p(v_p), conv_prompt,
            heads_s(k_s), heads_s(v_s), conv_sample)
```
